```python
import math
import jax, jax.numpy as jnp
from jax import lax
import numpy as np

D_MODEL = 4096
BATCH = 2
SEQ = 8192
DEPTH = 2

CHUNK = 64
N_META = 16
N_BRANCH = 3
BRANCH_WIDTH = D_MODEL // 2
EPS = 1e-6

SSD_HEAD_DIM = 64
SSD_HEADS = BRANCH_WIDTH // SSD_HEAD_DIM
SSD_GROUPS = 4
SSD_HEADS_PER_GROUP = SSD_HEADS // SSD_GROUPS
SSD_STATE = 128
SSD_CONV = 4
SSD_CONV_DIM = BRANCH_WIDTH + 2 * SSD_GROUPS * SSD_STATE

MLA_HEADS = 16
MLA_NOPE = 128
MLA_ROPE = 64
MLA_V = BRANCH_WIDTH // MLA_HEADS
MLA_Q_RANK = D_MODEL // 4
MLA_KV_RANK = 512
ROPE_BASE = 10000.0
Q_BLOCK = 128

POOL_WINDOWS = (2, 4, 8, 16)
POOL_GROUPS = 4
POOL_GROUP_DIM = BRANCH_WIDTH // POOL_GROUPS

IN_SIZES = (
    BRANCH_WIDTH,
    SSD_CONV_DIM,
    SSD_HEADS,
    MLA_Q_RANK,
    MLA_KV_RANK,
    MLA_ROPE,
    BRANCH_WIDTH,
    BRANCH_WIDTH,
    BRANCH_WIDTH,
)
IN_DIM = sum(IN_SIZES)

kernel_name = "hybrid_ssd_mla_pool_streaming_trunk"


def rms_norm(x, w):
    xf = x.astype(jnp.float32)
    y = xf * lax.rsqrt(jnp.mean(xf * xf, axis=-1, keepdims=True) + EPS)
    return (y * w.astype(jnp.float32)).astype(x.dtype)


def chunk_ids(m):
    p = jnp.arange(m)
    return jnp.where(p < N_META, 0, (p - N_META) // CHUNK + 1)


def rotary(x, cos, sin):
    x1, x2 = jnp.split(x, 2, axis=-1)
    return jnp.concatenate([x1 * cos - x2 * sin, x2 * cos + x1 * sin], axis=-1)


def ssd_mixer(z, xbc, dt_raw, conv_w, conv_b, dt_bias, a_log, d_skip, norm_w):
    b, n, _ = xbc.shape
    f32 = jnp.float32
    xbc = lax.conv_general_dilated(
        xbc, conv_w, window_strides=(1,), padding=[(SSD_CONV - 1, 0)],
        dimension_numbers=("NWC", "WIO", "NWC"), feature_group_count=SSD_CONV_DIM)
    xbc = jax.nn.silu(xbc + conv_b)
    xs, bm, cm = jnp.split(xbc, [BRANCH_WIDTH, BRANCH_WIDTH + SSD_GROUPS * SSD_STATE], axis=-1)
    xs = xs.reshape(b, n, SSD_HEADS, SSD_HEAD_DIM).astype(f32)
    dt = jax.nn.softplus(dt_raw.astype(f32) + dt_bias.astype(f32))
    da = dt * (-jnp.exp(a_log.astype(f32)))
    pad = CHUNK - N_META
    nc = (n + pad) // CHUNK

    def chunked(t):
        t = jnp.pad(t, ((0, 0), (pad, 0)) + ((0, 0),) * (t.ndim - 2))
        return t.reshape((b, nc, CHUNK) + t.shape[2:])

    xdt = chunked(xs * dt[..., None]).reshape(
        b, nc, CHUNK, SSD_GROUPS, SSD_HEADS_PER_GROUP, SSD_HEAD_DIM)
    bc = chunked(bm.astype(f32).reshape(b, n, SSD_GROUPS, SSD_STATE))
    cc = chunked(cm.astype(f32).reshape(b, n, SSD_GROUPS, SSD_STATE))
    a_c = chunked(da).reshape(b, nc, CHUNK, SSD_GROUPS, SSD_HEADS_PER_GROUP)
    a_c = a_c.transpose(0, 3, 4, 1, 2)
    a_cum = jnp.cumsum(a_c, axis=-1)
    causal = jnp.tril(jnp.ones((CHUNK, CHUNK), dtype=bool))
    seg = a_cum[..., :, None] - a_cum[..., None, :]
    decay_in = jnp.exp(jnp.where(causal, seg, -jnp.inf))
    y_diag = jnp.einsum("bclgn,bcsgn,bgrcls,bcsgrp->bclgrp", cc, bc, decay_in, xdt)
    decay_to_end = jnp.exp(a_cum[..., -1:] - a_cum)
    chunk_states = jnp.einsum("bclgn,bgrcl,bclgrp->cbgrpn", bc, decay_to_end, xdt)
    chunk_decay = jnp.exp(a_cum[..., -1]).transpose(3, 0, 1, 2)

    def step(state, inp):
        s_c, d_c = inp
        return state * d_c[..., None, None] + s_c, state

    init = jnp.zeros(chunk_states.shape[1:], f32)
    _, start_states = lax.scan(step, init, (chunk_states, chunk_decay))
    y_off = jnp.einsum("bclgn,cbgrpn,bgrcl->bclgrp", cc, start_states, jnp.exp(a_cum))
    y = (y_diag + y_off).reshape(b, nc * CHUNK, SSD_HEADS, SSD_HEAD_DIM)[:, pad:]
    y = y + d_skip.astype(f32)[:, None] * xs
    y = y.reshape(b, n, BRANCH_WIDTH) * jax.nn.silu(z.astype(f32))
    yg = y.reshape(b, n, SSD_GROUPS, BRANCH_WIDTH // SSD_GROUPS)
    yg = yg * lax.rsqrt(jnp.mean(yg * yg, axis=-1, keepdims=True) + EPS)
    return (yg.reshape(b, n, BRANCH_WIDTH) * norm_w.astype(f32)).astype(z.dtype)


def mla_mixer(q_lat, kv_lat, k_rope_raw, gate, q_norm_w, w_q_b, kv_norm_w, w_kv_b):
    b, n, _ = q_lat.shape
    pos = jnp.arange(n, dtype=jnp.float32)
    inv_freq = jnp.power(ROPE_BASE, -jnp.arange(0, MLA_ROPE, 2, dtype=jnp.float32) / MLA_ROPE)
    ang = pos[:, None] * inv_freq[None, :]
    cos = jnp.cos(ang).astype(q_lat.dtype)
    sin = jnp.sin(ang).astype(q_lat.dtype)
    q = (rms_norm(q_lat, q_norm_w) @ w_q_b).reshape(b, n, MLA_HEADS, MLA_NOPE + MLA_ROPE)
    q_nope, q_pe = jnp.split(q, [MLA_NOPE], axis=-1)
    q_pe = rotary(q_pe, cos[:, None, :], sin[:, None, :])
    kv = (rms_norm(kv_lat, kv_norm_w) @ w_kv_b).reshape(b, n, MLA_HEADS, MLA_NOPE + MLA_V)
    k_nope, v = jnp.split(kv, [MLA_NOPE], axis=-1)
    k_pe = rotary(k_rope_raw, cos, sin)
    scale = 1.0 / math.sqrt(MLA_NOPE + MLA_ROPE)
    nq = -(-n // Q_BLOCK) * Q_BLOCK
    nblk = nq // Q_BLOCK
    qpad = nq - n

    def blocks(t):
        t = jnp.pad(t, ((0, 0), (0, qpad), (0, 0), (0, 0)))
        return t.reshape((b, nblk, Q_BLOCK) + t.shape[2:]).swapaxes(0, 1)

    qn_blk = blocks(q_nope)
    qp_blk = blocks(q_pe)
    qc_blk = chunk_ids(nq).reshape(nblk, Q_BLOCK)
    kc = chunk_ids(n)

    def attend(args):
        qn_b, qp_b, qc_b = args
        s = (jnp.einsum("bqhd,bkhd->bhqk", qn_b, k_nope, preferred_element_type=jnp.float32)
             + jnp.einsum("bqhd,bkd->bhqk", qp_b, k_pe, preferred_element_type=jnp.float32)) * scale
        s = jnp.where(qc_b[:, None] >= kc[None, :], s, -jnp.inf)
        p = jax.nn.softmax(s, axis=-1).astype(v.dtype)
        return jnp.einsum("bhqk,bkhd->bqhd", p, v)

    o = lax.map(attend, (qn_blk, qp_blk, qc_blk))
    o = o.swapaxes(0, 1).reshape(b, nq, MLA_HEADS * MLA_V)[:, :n]
    return o * jax.nn.silu(gate)


def pool_mixer(u, gate, w_pool, pool_scale):
    b, n, _ = u.shape
    uf = u.astype(jnp.float32)
    csum = jnp.cumsum(uf, axis=1)
    steps = jnp.arange(1, n + 1)
    pooled = []
    for g, w in enumerate(POOL_WINDOWS):
        cg = csum[..., g * POOL_GROUP_DIM:(g + 1) * POOL_GROUP_DIM]
        lag = jnp.pad(cg, ((0, 0), (w, 0), (0, 0)))[:, :n]
        count = jnp.minimum(steps, w).astype(jnp.float32)[None, :, None]
        pooled.append((cg - lag) / count)
    pooled = jnp.stack(pooled, axis=2)
    mixed = (pooled - uf.reshape(b, n, POOL_GROUPS, POOL_GROUP_DIM)).astype(u.dtype)
    y = jnp.einsum("bngc,gcd->bngd", mixed, w_pool).reshape(b, n, BRANCH_WIDTH) * pool_scale
    return y * jax.nn.silu(gate)


def hybrid_layer(h, pre_w, w_in, conv_w, conv_b, dt_bias, a_log, d_skip, ssd_norm_w,
                 q_norm_w, w_q_b, kv_norm_w, w_kv_b, w_pool, pool_scale,
                 w_gate, w_branch, w_out, post_w):
    xn = rms_norm(h, pre_w)
    proj = xn @ w_in
    offsets = [int(o) for o in np.cumsum(IN_SIZES)[:-1]]
    (z, xbc, dt_raw, q_lat, kv_lat, k_rope, mla_gate, pool_u, pool_gate) = jnp.split(proj, offsets, axis=-1)
    y_ssd = ssd_mixer(z, xbc, dt_raw, conv_w, conv_b, dt_bias, a_log, d_skip, ssd_norm_w)
    y_mla = mla_mixer(q_lat, kv_lat, k_rope, mla_gate, q_norm_w, w_q_b, kv_norm_w, w_kv_b)
    y_pool = pool_mixer(pool_u, pool_gate, w_pool, pool_scale)
    branches = (y_ssd, y_mla, y_pool)
    merged = jax.nn.sigmoid(xn @ w_gate[0]) * (branches[0] @ w_branch[0])
    for i in range(1, N_BRANCH):
        merged = merged + jax.nn.sigmoid(xn @ w_gate[i]) * (branches[i] @ w_branch[i])
    out = merged @ w_out
    return h + rms_norm(out, post_w)


def setup_inputs(seed: int = 0) -> dict:
    key = jax.random.key(seed)
    ks = jax.random.split(key, 20)
    f32 = jnp.float32

    def nrm(k, shape, scale):
        return jax.random.normal(k, shape, f32) * scale

    def gain(k, shape):
        return 1.0 + 0.02 * jax.random.normal(k, shape, f32)

    dt0 = jnp.exp(jax.random.uniform(ks[6], (DEPTH, SSD_HEADS), f32, math.log(1e-3), math.log(1e-1)))
    dt_bias = dt0 + jnp.log(-jnp.expm1(-dt0))
    return {
        "x": nrm(ks[0], (BATCH, SEQ, D_MODEL), 1.0),
        "meta_tokens": nrm(ks[1], (N_META, D_MODEL), 1.0),
        "pre_norm_w": gain(ks[2], (DEPTH, D_MODEL)),
        "w_in": nrm(ks[3], (DEPTH, D_MODEL, IN_DIM), D_MODEL ** -0.5),
        "conv_w": nrm(ks[4], (DEPTH, SSD_CONV, 1, SSD_CONV_DIM), SSD_CONV ** -0.5),
        "conv_b": nrm(ks[5], (DEPTH, SSD_CONV_DIM), 0.02),
        "dt_bias": dt_bias,
        "a_log": jnp.log(jax.random.uniform(ks[7], (DEPTH, SSD_HEADS), f32, 1.0, 16.0)),
        "d_skip": gain(ks[8], (DEPTH, SSD_HEADS)),
        "ssd_norm_w": gain(ks[9], (DEPTH, BRANCH_WIDTH)),
        "q_norm_w": gain(ks[10], (DEPTH, MLA_Q_RANK)),
        "w_q_b": nrm(ks[11], (DEPTH, MLA_Q_RANK, MLA_HEADS * (MLA_NOPE + MLA_ROPE)), MLA_Q_RANK ** -0.5),
        "kv_norm_w": gain(ks[12], (DEPTH, MLA_KV_RANK)),
        "w_kv_b": nrm(ks[13], (DEPTH, MLA_KV_RANK, MLA_HEADS * (MLA_NOPE + MLA_V)), MLA_KV_RANK ** -0.5),
        "w_pool": nrm(ks[14], (DEPTH, POOL_GROUPS, POOL_GROUP_DIM, POOL_GROUP_DIM), POOL_GROUP_DIM ** -0.5),
        "pool_scale": gain(ks[15], (DEPTH, BRANCH_WIDTH)),
        "w_gate": nrm(ks[16], (DEPTH, N_BRANCH, D_MODEL, D_MODEL), D_MODEL ** -0.5),
        "w_branch": nrm(ks[17], (DEPTH, N_BRANCH, BRANCH_WIDTH, D_MODEL), BRANCH_WIDTH ** -0.5),
        "w_out": nrm(ks[18], (DEPTH, D_MODEL, D_MODEL), D_MODEL ** -0.5),
        "post_norm_w": gain(ks[19], (DEPTH, D_MODEL)),
    }


def reference(x, meta_tokens, pre_norm_w, w_in, conv_w, conv_b, dt_bias, a_log, d_skip,
              ssd_norm_w, q_norm_w, w_q_b, kv_norm_w, w_kv_b, w_pool, pool_scale,
              w_gate, w_branch, w_out, post_norm_w):
    b = x.shape[0]
    meta = jnp.broadcast_to(meta_tokens.astype(x.dtype)[None], (b, N_META, D_MODEL))
    h = jnp.concatenate([meta, x], axis=1)
    for i in range(DEPTH):
        h = hybrid_layer(h, pre_norm_w[i], w_in[i], conv_w[i], conv_b[i], dt_bias[i], a_log[i],
                         d_skip[i], ssd_norm_w[i], q_norm_w[i], w_q_b[i], kv_norm_w[i], w_kv_b[i],
                         w_pool[i], pool_scale[i], w_gate[i], w_branch[i], w_out[i], post_norm_w[i])
    return h[:, N_META:]
```

```python
import functools
import math

import jax
import jax.numpy as jnp
from jax import lax
from jax.experimental import pallas as pl
from jax.experimental.pallas import tpu as pltpu

F32 = jnp.float32
BF16 = jnp.bfloat16

D_MODEL = 4096
N_META = 16
BRANCH_WIDTH = D_MODEL // 2
EPS = 1e-6

SSD_HEAD_DIM = 64
SSD_HEADS = BRANCH_WIDTH // SSD_HEAD_DIM
SSD_GROUPS = 4
SSD_HEADS_PER_GROUP = SSD_HEADS // SSD_GROUPS
SSD_STATE = 128
SSD_CONV = 4
SSD_BC = 2 * SSD_GROUPS * SSD_STATE
SSD_GROUP_WIDTH = BRANCH_WIDTH // SSD_GROUPS

MLA_HEADS = 16
MLA_NOPE = 128
MLA_ROPE = 64
MLA_V = BRANCH_WIDTH // MLA_HEADS
MLA_Q_RANK = D_MODEL // 4
MLA_KV_RANK = 512
ROPE_BASE = 10000.0
CHUNK = 64

POOL_WINDOWS = (2, 4, 8, 16)
POOL_GROUP_DIM = BRANCH_WIDTH // len(POOL_WINDOWS)
POOL_HALO = 16
CONV_HALO = 8

LANES = 128
MLA_QK_PAD = 256
MLA_Q_COLS = MLA_QK_PAD + LANES

P_Z = 0
P_X = 2048
P_MLA_GATE = 4096
P_POOL_U = 6144
P_POOL_GATE = 8192
P_BC = 10240
P_QLAT = 11264
P_KVLAT = 12288
P_GATES = 12800
P_WIDTH = P_GATES + 3 * D_MODEL
SMALL_WIDTH = 3 * LANES

META_PAD = 1024
TM_NORM = 256
TM_PROJ = 1024
TN_PROJ = 512
SSD_L = 256
TP_POOL = 512
TM_MLA = 256
TQ = 512
TK_META = 128
TM_MERGE = 512
TN_MERGE = 512
TM_OUT = 1024
TN_OUT = 512

VMEM_LIMIT = 56 * 1024 * 1024


def _cparams(sem):
    return pltpu.CompilerParams(dimension_semantics=sem, vmem_limit_bytes=VMEM_LIMIT)


def _silu(x):
    return x * jax.nn.sigmoid(x)


def _split2(x):
    hi = x.astype(BF16)
    lo = (x - hi.astype(F32)).astype(BF16)
    return hi, lo


def _split3(x):
    hi = x.astype(BF16)
    r = x - hi.astype(F32)
    mid = r.astype(BF16)
    lo = (r - mid.astype(F32)).astype(BF16)
    return hi, mid, lo


def _dot(a, b):
    return jnp.dot(a, b, preferred_element_type=F32)


def _dot_nt(a, b):
    return lax.dot_general(a, b, (((1,), (1,)), ((), ())), preferred_element_type=F32)


def _rmsnorm_kernel(h_ref, w_ref, o_ref):
    x = h_ref[...]
    ms = jnp.mean(x * x, axis=-1, keepdims=True)
    o_ref[...] = (x * lax.rsqrt(ms + EPS) * w_ref[...]).astype(o_ref.dtype)


def _rmsnorm(h, w):
    rows, d = h.shape
    return pl.pallas_call(
        _rmsnorm_kernel,
        out_shape=jax.ShapeDtypeStruct((rows, d), BF16),
        grid=(rows // TM_NORM,),
        in_specs=[pl.BlockSpec((TM_NORM, d), lambda i: (i, 0)),
                  pl.BlockSpec((1, d), lambda i: (0, 0))],
        out_specs=pl.BlockSpec((TM_NORM, d), lambda i: (i, 0)),
        compiler_params=_cparams(("parallel",)),
        name="pre_rmsnorm",
    )(h, w)


def _proj_kernel(a_ref, w_ref, ws_ref, o_ref, os_ref, *, gate_tile0):
    j = pl.program_id(1)
    acc = _dot(a_ref[...], w_ref[...])

    @pl.when(j < gate_tile0)
    def _():
        o_ref[...] = acc.astype(o_ref.dtype)

    @pl.when(j >= gate_tile0)
    def _():
        o_ref[...] = jax.nn.sigmoid(acc).astype(o_ref.dtype)

    @pl.when(j == 0)
    def _():
        os_ref[...] = _dot(a_ref[...], ws_ref[...])


def _proj(xn, w_big, w_small):
    rows, d = xn.shape
    n = w_big.shape[1]
    return pl.pallas_call(
        functools.partial(_proj_kernel, gate_tile0=P_GATES // TN_PROJ),
        out_shape=(jax.ShapeDtypeStruct((rows, n), BF16),
                   jax.ShapeDtypeStruct((rows, SMALL_WIDTH), F32)),
        grid=(rows // TM_PROJ, n // TN_PROJ),
        in_specs=[pl.BlockSpec((TM_PROJ, d), lambda i, j: (i, 0)),
                  pl.BlockSpec((d, TN_PROJ), lambda i, j: (0, j)),
                  pl.BlockSpec((d, SMALL_WIDTH), lambda i, j: (0, 0))],
        out_specs=(pl.BlockSpec((TM_PROJ, TN_PROJ), lambda i, j: (i, j)),
                   pl.BlockSpec((TM_PROJ, SMALL_WIDTH), lambda i, j: (i, 0))),
        compiler_params=_cparams(("parallel", "arbitrary")),
        name="in_proj",
    )(xn, w_big, w_small)


def _seq_steps(batch, seq, tile):
    frame_steps = batch * (seq // tile)
    return frame_steps, 1 + frame_steps + META_PAD // tile - 1


def _seq_row_block(s, batch, seq, tile):
    frame_steps = batch * (seq // tile)
    meta_blk = (batch * seq) // tile
    return jnp.where(s == 0, meta_blk, jnp.where(s <= frame_steps, s - 1, meta_blk + s - frame_steps))


def _ssd_kernel(z_ref, x_ref, bc_ref, dt_ref, cwx_ref, cwbc_ref, cbx_ref, cbbc_ref, hp_ref,
                dskip_ref, normw_ref, expand_ref, tri_ref, o_ref,
                state_ref, tailx_ref, tailbc_ref, mstate_ref, mtailx_ref, mtailbc_ref,
                padx_ref, padbc_ref, y_ref, *, steps_per_batch, frame_steps):
    L = SSD_L
    s = pl.program_id(0)
    is_meta = s == 0
    active = s <= frame_steps
    first_of_batch = jnp.logical_and(jnp.logical_and(active, s > 0), lax.rem(s - 1, steps_per_batch) == 0)

    @pl.when(jnp.logical_not(active))
    def _():
        o_ref[...] = jnp.zeros_like(o_ref)

    @pl.when(is_meta)
    def _():
        state_ref[...] = jnp.zeros_like(state_ref)
        tailx_ref[...] = jnp.zeros_like(tailx_ref)
        tailbc_ref[...] = jnp.zeros_like(tailbc_ref)

    @pl.when(first_of_batch)
    def _():
        state_ref[...] = mstate_ref[...]
        tailx_ref[...] = mtailx_ref[...]
        tailbc_ref[...] = mtailbc_ref[...]

    @pl.when(active)
    def _():
        padx_ref[0:CONV_HALO, :] = tailx_ref[...]
        padx_ref[CONV_HALO:CONV_HALO + L, :] = x_ref[...].astype(F32)
        padbc_ref[0:CONV_HALO, :] = tailbc_ref[...]
        padbc_ref[CONV_HALO:CONV_HALO + L, :] = bc_ref[...].astype(F32)

        @pl.when(is_meta)
        def _():
            tailx_ref[...] = padx_ref[N_META:N_META + CONV_HALO, :]
            tailbc_ref[...] = padbc_ref[N_META:N_META + CONV_HALO, :]
            mtailx_ref[...] = padx_ref[N_META:N_META + CONV_HALO, :]
            mtailbc_ref[...] = padbc_ref[N_META:N_META + CONV_HALO, :]

        @pl.when(jnp.logical_not(is_meta))
        def _():
            tailx_ref[...] = padx_ref[L:L + CONV_HALO, :]
            tailbc_ref[...] = padbc_ref[L:L + CONV_HALO, :]

        def conv(pad_ref, w_ref, b_ref):
            acc = b_ref[...] + w_ref[0:1, :] * pad_ref[pl.ds(CONV_HALO - 3, L), :]
            for k in range(1, SSD_CONV):
                acc = acc + w_ref[k:k + 1, :] * pad_ref[pl.ds(CONV_HALO - 3 + k, L), :]
            return _silu(acc)

        xs = conv(padx_ref, cwx_ref, cbx_ref)
        bcm = conv(padbc_ref, cwbc_ref, cbbc_ref)

        row = lax.broadcasted_iota(jnp.int32, (L, LANES), 0)
        lane = lax.broadcasted_iota(jnp.int32, (L, LANES), 1)
        valid = jnp.logical_and(lane < SSD_HEADS, jnp.logical_or(jnp.logical_not(is_meta), row < N_META))
        pre = dt_ref[...] + hp_ref[0:1, :]
        dt = jnp.maximum(pre, 0.0) + jnp.log(1.0 + jnp.exp(-jnp.abs(pre)))
        dt = jnp.where(valid, dt, 0.0)
        a = dt * (-jnp.exp(hp_ref[1:2, :]))

        tri = tri_ref[...]
        a_cum = sum(_dot(tri, part) for part in _split3(a))
        a_cum_t = a_cum.T
        a_last = a_cum[L - 1:L, :]
        ea = jnp.exp(a_cum)
        dte = jnp.exp(a_last - a_cum)
        dec_tot = jnp.broadcast_to(jnp.exp(a_last), (8, LANES))

        expand = expand_ref[...]

        def widen(v):
            return sum(_dot(part, expand) for part in _split2(v))

        dt_w = widen(dt)
        ea_w = widen(ea)
        dte_w = widen(dte)
        dec_tot_w = widen(dec_tot)[0:1, :]

        xdt = xs * dt_w
        xdt_bf = xdt.astype(BF16)
        xdte_bf = (xdt * dte_w).astype(BF16)

        r_i = lax.broadcasted_iota(jnp.int32, (L, L), 0)
        c_i = lax.broadcasted_iota(jnp.int32, (L, L), 1)
        causal = r_i >= c_i
        lane_lo = lane < SSD_HEAD_DIM

        for g in range(SSD_GROUPS):
            b_g = bcm[:, g * SSD_STATE:(g + 1) * SSD_STATE]
            c_off = SSD_GROUPS * SSD_STATE + g * SSD_STATE
            c_g = bcm[:, c_off:c_off + SSD_STATE]
            b_bf = b_g.astype(BF16)
            c_bf = c_g.astype(BF16)
            cb = _dot_nt(c_bf, b_bf)
            for pair in range(SSD_HEADS_PER_GROUP // 2):
                ys = []
                for half in range(2):
                    h = g * SSD_HEADS_PER_GROUP + 2 * pair + half
                    seg = a_cum[:, h:h + 1] - a_cum_t[h:h + 1, :]
                    dec = jnp.exp(jnp.where(causal, seg, -jnp.inf))
                    m = (cb * dec).astype(BF16)
                    c0 = (h // 2) * LANES
                    ys.append(_dot(m, xdt_bf[:, c0:c0 + LANES]))
                c0 = (g * SSD_HEADS_PER_GROUP // 2 + pair) * LANES
                y_ref[:, c0:c0 + LANES] = jnp.where(lane_lo, ys[0], ys[1])
            gs = slice(g * SSD_GROUP_WIDTH, (g + 1) * SSD_GROUP_WIDTH)
            st = state_ref[g]
            y_off = _dot(c_bf, st.astype(BF16)) * ea_w[:, gs]
            y_ref[:, gs] = y_ref[:, gs] + y_off
            state_ref[g] = st * dec_tot_w[:, gs] + _dot(b_g.T.astype(BF16), xdte_bf[:, gs])

        @pl.when(is_meta)
        def _():
            mstate_ref[...] = state_ref[...]

        y = y_ref[...] + dskip_ref[...] * xs
        y = y * _silu(z_ref[...].astype(F32))
        for g in range(SSD_GROUPS):
            gs = slice(g * SSD_GROUP_WIDTH, (g + 1) * SSD_GROUP_WIDTH)
            yg = y[:, gs]
            yg = yg * lax.rsqrt(jnp.mean(yg * yg, axis=-1, keepdims=True) + EPS)
            o_ref[:, gs] = (yg * normw_ref[:, gs]).astype(o_ref.dtype)


def _ssd(p, small, conv_w, conv_b, dt_bias, a_log, d_skip, norm_w, batch, seq):
    rows = p.shape[0]
    L = SSD_L
    frame_steps, n_steps = _seq_steps(batch, seq, L)
    rb = functools.partial(_seq_row_block, batch=batch, seq=seq, tile=L)

    cw = conv_w.reshape(SSD_CONV, -1).astype(F32)
    cb = conv_b.reshape(1, -1).astype(F32)
    hp = jnp.zeros((8, LANES), F32)
    hp = hp.at[0, :SSD_HEADS].set(dt_bias.astype(F32)).at[1, :SSD_HEADS].set(a_log.astype(F32))
    dskip_w = jnp.repeat(d_skip.astype(F32), SSD_HEAD_DIM).reshape(1, BRANCH_WIDTH)
    expand = (jnp.arange(LANES)[:, None] == (jnp.arange(BRANCH_WIDTH) // SSD_HEAD_DIM)[None, :]).astype(BF16)
    tri = (jnp.arange(L)[:, None] >= jnp.arange(L)[None, :]).astype(BF16)

    def const(shape):
        return pl.BlockSpec(shape, lambda s: (0,) * len(shape))

    state_shape = (SSD_GROUPS, SSD_STATE, SSD_GROUP_WIDTH)
    return pl.pallas_call(
        functools.partial(_ssd_kernel, steps_per_batch=seq // L, frame_steps=frame_steps),
        out_shape=jax.ShapeDtypeStruct((rows, BRANCH_WIDTH), BF16),
        grid=(n_steps,),
        in_specs=[pl.BlockSpec((L, BRANCH_WIDTH), lambda s: (rb(s), P_Z // BRANCH_WIDTH)),
                  pl.BlockSpec((L, BRANCH_WIDTH), lambda s: (rb(s), P_X // BRANCH_WIDTH)),
                  pl.BlockSpec((L, SSD_BC), lambda s: (rb(s), P_BC // SSD_BC)),
                  pl.BlockSpec((L, LANES), lambda s: (rb(s), 0)),
                  const((SSD_CONV, BRANCH_WIDTH)), const((SSD_CONV, SSD_BC)),
                  const((1, BRANCH_WIDTH)), const((1, SSD_BC)),
                  const((8, LANES)), const((1, BRANCH_WIDTH)), const((1, BRANCH_WIDTH)),
                  const((LANES, BRANCH_WIDTH)), const((L, L))],
        out_specs=pl.BlockSpec((L, BRANCH_WIDTH), lambda s: (rb(s), 0)),
        scratch_shapes=[pltpu.VMEM(state_shape, F32),
                        pltpu.VMEM((CONV_HALO, BRANCH_WIDTH), F32),
                        pltpu.VMEM((CONV_HALO, SSD_BC), F32),
                        pltpu.VMEM(state_shape, F32),
                        pltpu.VMEM((CONV_HALO, BRANCH_WIDTH), F32),
                        pltpu.VMEM((CONV_HALO, SSD_BC), F32),
                        pltpu.VMEM((L + CONV_HALO, BRANCH_WIDTH), F32),
                        pltpu.VMEM((L + CONV_HALO, SSD_BC), F32),
                        pltpu.VMEM((L, BRANCH_WIDTH), F32)],
        compiler_params=_cparams(("arbitrary",)),
        name="ssd_mixer",
    )(p, p, p, small, cw[:, :BRANCH_WIDTH], cw[:, BRANCH_WIDTH:], cb[:, :BRANCH_WIDTH], cb[:, BRANCH_WIDTH:],
      hp, dskip_w, norm_w.reshape(1, -1).astype(F32), expand, tri)


def _pool_kernel(u_ref, gate_ref, w_ref, scale_ref, o_ref, tail_ref, mtail_ref, pad_ref,
                 *, steps_per_batch, frame_steps):
    T = TP_POOL
    s = pl.program_id(0)
    is_meta = s == 0
    active = s <= frame_steps
    first_of_batch = jnp.logical_and(jnp.logical_and(active, s > 0), lax.rem(s - 1, steps_per_batch) == 0)

    @pl.when(jnp.logical_not(active))
    def _():
        o_ref[...] = jnp.zeros_like(o_ref)

    @pl.when(is_meta)
    def _():
        tail_ref[...] = jnp.zeros_like(tail_ref)

    @pl.when(first_of_batch)
    def _():
        tail_ref[...] = mtail_ref[...]

    @pl.when(active)
    def _():
        pad_ref[0:POOL_HALO, :] = tail_ref[...]
        pad_ref[POOL_HALO:POOL_HALO + T, :] = u_ref[...].astype(F32)

        @pl.when(is_meta)
        def _():
            mtail_ref[...] = pad_ref[POOL_HALO:POOL_HALO + N_META, :]

        @pl.when(jnp.logical_not(is_meta))
        def _():
            tail_ref[...] = pad_ref[T:T + POOL_HALO, :]

        row = lax.broadcasted_iota(jnp.int32, (T, 1), 0)
        for g, w in enumerate(POOL_WINDOWS):
            cs = slice(g * POOL_GROUP_DIM, (g + 1) * POOL_GROUP_DIM)
            u_g = pad_ref[pl.ds(POOL_HALO, T), cs]
            acc = u_g
            for k in range(1, w):
                acc = acc + pad_ref[pl.ds(POOL_HALO - k, T), cs]
            count = jnp.where(is_meta, jnp.minimum(row + 1, w), w).astype(F32)
            mixed = acc / count - u_g
            y = _dot(mixed.astype(BF16), w_ref[g])
            y = y * scale_ref[:, cs] * _silu(gate_ref[:, cs].astype(F32))
            o_ref[:, cs] = y.astype(o_ref.dtype)


def _pool(p, w_pool, pool_scale, batch, seq):
    rows = p.shape[0]
    T = TP_POOL
    frame_steps, n_steps = _seq_steps(batch, seq, T)
    rb = functools.partial(_seq_row_block, batch=batch, seq=seq, tile=T)
    return pl.pallas_call(
        functools.partial(_pool_kernel, steps_per_batch=seq // T, frame_steps=frame_steps),
        out_shape=jax.ShapeDtypeStruct((rows, BRANCH_WIDTH), BF16),
        grid=(n_steps,),
        in_specs=[pl.BlockSpec((T, BRANCH_WIDTH), lambda s: (rb(s), P_POOL_U // BRANCH_WIDTH)),
                  pl.BlockSpec((T, BRANCH_WIDTH), lambda s: (rb(s), P_POOL_GATE // BRANCH_WIDTH)),
                  pl.BlockSpec((len(POOL_WINDOWS), POOL_GROUP_DIM, POOL_GROUP_DIM), lambda s: (0, 0, 0)),
                  pl.BlockSpec((1, BRANCH_WIDTH), lambda s: (0, 0))],
        out_specs=pl.BlockSpec((T, BRANCH_WIDTH), lambda s: (rb(s), 0)),
        scratch_shapes=[pltpu.VMEM((POOL_HALO, BRANCH_WIDTH), F32),
                        pltpu.VMEM((POOL_HALO, BRANCH_WIDTH), F32),
                        pltpu.VMEM((T + POOL_HALO, BRANCH_WIDTH), F32)],
        compiler_params=_cparams(("arbitrary",)),
        name="pool_mixer",
    )(p, p, w_pool.astype(BF16), pool_scale.reshape(1, -1).astype(F32))


def _mla_proj_kernel(ql_ref, kvl_ref, small_ref, cos_ref, sin_ref, qnw_ref, kvnw_ref, wq_ref, wkv_ref,
                     q_ref, k_ref, v_ref):
    def norm(x, w):
        ms = jnp.mean(x * x, axis=-1, keepdims=True)
        return (x * lax.rsqrt(ms + EPS) * w).astype(BF16)

    xq = norm(ql_ref[...].astype(F32), qnw_ref[...])
    xkv = norm(kvl_ref[...].astype(F32), kvnw_ref[...])
    cosv = cos_ref[...]
    sinv = sin_ref[...]
    scale = 1.0 / math.sqrt(MLA_NOPE + MLA_ROPE)
    k_pe = (small_ref[:, LANES:2 * LANES] * cosv + small_ref[:, 2 * LANES:3 * LANES] * sinv).astype(BF16)
    for h in range(MLA_HEADS):
        aq = _dot(xq, wq_ref[:, h * MLA_Q_COLS:(h + 1) * MLA_Q_COLS])
        c0 = h * MLA_QK_PAD
        q_ref[:, c0:c0 + LANES] = (aq[:, 0:LANES] * scale).astype(BF16)
        q_pe = aq[:, LANES:2 * LANES] * cosv + aq[:, 2 * LANES:3 * LANES] * sinv
        q_ref[:, c0 + LANES:c0 + 2 * LANES] = (q_pe * scale).astype(BF16)
        akv = _dot(xkv, wkv_ref[:, h * 2 * LANES:(h + 1) * 2 * LANES])
        k_ref[:, c0:c0 + LANES] = akv[:, 0:LANES].astype(BF16)
        k_ref[:, c0 + LANES:c0 + 2 * LANES] = k_pe
        v_ref[:, h * MLA_V:(h + 1) * MLA_V] = akv[:, LANES:2 * LANES].astype(BF16)


def _mla_proj(p, small, cosv, sinv, q_norm_w, kv_norm_w, wq, wkv):
    rows = p.shape[0]
    T = TM_MLA
    nq = MLA_HEADS * MLA_QK_PAD

    def const(shape):
        return pl.BlockSpec(shape, lambda i: (0,) * len(shape))

    return pl.pallas_call(
        _mla_proj_kernel,
        out_shape=(jax.ShapeDtypeStruct((rows, nq), BF16),
                   jax.ShapeDtypeStruct((rows, nq), BF16),
                   jax.ShapeDtypeStruct((rows, BRANCH_WIDTH), BF16)),
        grid=(rows // T,),
        in_specs=[pl.BlockSpec((T, MLA_Q_RANK), lambda i: (i, P_QLAT // MLA_Q_RANK)),
                  pl.BlockSpec((T, MLA_KV_RANK), lambda i: (i, P_KVLAT // MLA_KV_RANK)),
                  pl.BlockSpec((T, SMALL_WIDTH), lambda i: (i, 0)),
                  pl.BlockSpec((T, LANES), lambda i: (i, 0)),
                  pl.BlockSpec((T, LANES), lambda i: (i, 0)),
                  const((1, MLA_Q_RANK)), const((1, MLA_KV_RANK)),
                  const((MLA_Q_RANK, MLA_HEADS * MLA_Q_COLS)),
                  const((MLA_KV_RANK, MLA_HEADS * 2 * LANES))],
        out_specs=(pl.BlockSpec((T, nq), lambda i: (i, 0)),
                   pl.BlockSpec((T, nq), lambda i: (i, 0)),
                   pl.BlockSpec((T, BRANCH_WIDTH), lambda i: (i, 0))),
        compiler_params=_cparams(("parallel",)),
        name="mla_proj",
    )(p, p, small, cosv, sinv, q_norm_w.reshape(1, -1).astype(F32), kv_norm_w.reshape(1, -1).astype(F32), wq, wkv)


def _attn_kernel(q_ref, k_ref, v_ref, km_ref, vm_ref, gate_ref, o_ref, *, steps_per_batch, frame_steps):
    s = pl.program_id(1)
    active = s <= frame_steps

    @pl.when(jnp.logical_not(active))
    def _():
        o_ref[...] = jnp.zeros_like(o_ref)

    @pl.when(active)
    def _():
        q = q_ref[...]

        s0 = _dot_nt(q, km_ref[...])
        col = lax.broadcasted_iota(jnp.int32, s0.shape, 1)
        s0 = jnp.where(col < N_META, s0, -jnp.inf)
        m0 = jnp.max(s0, axis=-1, keepdims=True)
        p0 = jnp.exp(s0 - m0)
        l0 = jnp.sum(p0, axis=-1, keepdims=True)
        acc0 = _dot(p0.astype(BF16), vm_ref[...])

        r_c = lax.broadcasted_iota(jnp.int32, (TQ, TQ), 0) // CHUNK
        c_c = lax.broadcasted_iota(jnp.int32, (TQ, TQ), 1) // CHUNK
        chunk_causal = r_c >= c_c

        def kv_step(j, carry, masked):
            m, l, acc = carry
            start = pl.multiple_of(j * TQ, TQ)
            k = k_ref[pl.ds(start, TQ), :]
            v = v_ref[pl.ds(start, TQ), :]
            sc = _dot_nt(q, k)
            if masked:
                sc = jnp.where(chunk_causal, sc, -jnp.inf)
            m_new = jnp.maximum(m, jnp.max(sc, axis=-1, keepdims=True))
            alpha = jnp.exp(m - m_new)
            p = jnp.exp(sc - m_new)
            l_new = alpha * l + jnp.sum(p, axis=-1, keepdims=True)
            acc_new = alpha * acc + _dot(p.astype(BF16), v)
            return m_new, l_new, acc_new

        is_frames = s > 0
        t = jnp.where(is_frames, lax.rem(s - 1, steps_per_batch), 0)
        carry = lax.fori_loop(0, t, lambda j, c: kv_step(j, c, False), (m0, l0, acc0))
        _, l1, acc1 = kv_step(t, carry, True)
        l = jnp.where(is_frames, l1, carry[1])
        acc = jnp.where(is_frames, acc1, carry[2])
        o = acc / l
        o_ref[...] = (o * _silu(gate_ref[...].astype(F32))).astype(o_ref.dtype)


def _attention(q, kc, v, p, batch, seq):
    rows = q.shape[0]
    steps = seq // TQ
    frame_steps, n_steps = _seq_steps(batch, seq, TQ)
    rb = functools.partial(_seq_row_block, batch=batch, seq=seq, tile=TQ)
    meta_k = (batch * seq) // TK_META

    def kvb(s):
        return jnp.clip((s - 1) // steps, 0, batch - 1)

    return pl.pallas_call(
        functools.partial(_attn_kernel, steps_per_batch=steps, frame_steps=frame_steps),
        out_shape=jax.ShapeDtypeStruct((rows, BRANCH_WIDTH), BF16),
        grid=(MLA_HEADS, n_steps),
        in_specs=[pl.BlockSpec((TQ, MLA_QK_PAD), lambda h, s: (rb(s), h)),
                  pl.BlockSpec((seq, MLA_QK_PAD), lambda h, s: (kvb(s), h)),
                  pl.BlockSpec((seq, MLA_V), lambda h, s: (kvb(s), h)),
                  pl.BlockSpec((TK_META, MLA_QK_PAD), lambda h, s: (meta_k, h)),
                  pl.BlockSpec((TK_META, MLA_V), lambda h, s: (meta_k, h)),
                  pl.BlockSpec((TQ, MLA_V), lambda h, s: (rb(s), P_MLA_GATE // MLA_V + h))],
        out_specs=pl.BlockSpec((TQ, MLA_V), lambda h, s: (rb(s), h)),
        compiler_params=_cparams(("arbitrary", "arbitrary")),
        name="mla_attention",
    )(q, kc, v, kc, v, p)


def _merge_kernel(ys_ref, ym_ref, yp_ref, w_ref, g0_ref, g1_ref, g2_ref, o_ref):
    acc = g0_ref[...].astype(F32) * _dot(ys_ref[...], w_ref[0])
    acc = acc + g1_ref[...].astype(F32) * _dot(ym_ref[...], w_ref[1])
    acc = acc + g2_ref[...].astype(F32) * _dot(yp_ref[...], w_ref[2])
    o_ref[...] = acc.astype(o_ref.dtype)


def _merge(y_ssd, y_mla, y_pool, w_branch, p):
    rows = p.shape[0]
    TM, TN = TM_MERGE, TN_MERGE
    g_blk = P_GATES // TN
    per = D_MODEL // TN
    y_spec = pl.BlockSpec((TM, BRANCH_WIDTH), lambda i, j: (i, 0))
    return pl.pallas_call(
        _merge_kernel,
        out_shape=jax.ShapeDtypeStruct((rows, D_MODEL), BF16),
        grid=(rows // TM, D_MODEL // TN),
        in_specs=[y_spec, y_spec, y_spec,
                  pl.BlockSpec((3, BRANCH_WIDTH, TN), lambda i, j: (0, 0, j)),
                  pl.BlockSpec((TM, TN), lambda i, j: (i, g_blk + j)),
                  pl.BlockSpec((TM, TN), lambda i, j: (i, g_blk + per + j)),
                  pl.BlockSpec((TM, TN), lambda i, j: (i, g_blk + 2 * per + j))],
        out_specs=pl.BlockSpec((TM, TN), lambda i, j: (i, j)),
        compiler_params=_cparams(("parallel", "arbitrary")),
        name="branch_merge",
    )(y_ssd, y_mla, y_pool, w_branch, p, p, p)


def _out_kernel(m_ref, w_ref, o_ref, ss_ref):
    j = pl.program_id(1)
    acc = _dot(m_ref[...], w_ref[...])
    o_ref[...] = acc
    sq = acc * acc
    part = sq[:, 0:LANES]
    for c in range(1, TN_OUT // LANES):
        part = part + sq[:, c * LANES:(c + 1) * LANES]

    @pl.when(j == 0)
    def _():
        ss_ref[...] = part

    @pl.when(j > 0)
    def _():
        ss_ref[...] = ss_ref[...] + part


def _out_proj(merged, w_out):
    rows = merged.shape[0]
    TM, TN = TM_OUT, TN_OUT
    return pl.pallas_call(
        _out_kernel,
        out_shape=(jax.ShapeDtypeStruct((rows, D_MODEL), F32),
                   jax.ShapeDtypeStruct((rows, LANES), F32)),
        grid=(rows // TM, D_MODEL // TN),
        in_specs=[pl.BlockSpec((TM, D_MODEL), lambda i, j: (i, 0)),
                  pl.BlockSpec((D_MODEL, TN), lambda i, j: (0, j))],
        out_specs=(pl.BlockSpec((TM, TN), lambda i, j: (i, j)),
                   pl.BlockSpec((TM, LANES), lambda i, j: (i, 0))),
        compiler_params=_cparams(("parallel", "arbitrary")),
        name="out_proj",
    )(merged, w_out)


def _post_kernel(h_ref, out_ref, ss_ref, pw_ref, nw_ref, hn_ref, *xn_ref):
    ms = jnp.sum(ss_ref[...], axis=-1, keepdims=True) * (1.0 / D_MODEL)
    hn = h_ref[...] + out_ref[...] * lax.rsqrt(ms + EPS) * pw_ref[...]
    hn_ref[...] = hn
    if xn_ref:
        ms2 = jnp.mean(hn * hn, axis=-1, keepdims=True)
        xn_ref[0][...] = (hn * lax.rsqrt(ms2 + EPS) * nw_ref[...]).astype(BF16)


def _post_norm_residual(h, out, ss, post_w, next_pre_w):
    rows, d = h.shape
    T = TM_NORM
    row_spec = pl.BlockSpec((T, d), lambda i: (i, 0))
    vec_spec = pl.BlockSpec((1, d), lambda i: (0, 0))
    with_next = next_pre_w is not None
    out_shape = [jax.ShapeDtypeStruct((rows, d), F32)]
    out_specs = [row_spec]
    if with_next:
        out_shape.append(jax.ShapeDtypeStruct((rows, d), BF16))
        out_specs.append(row_spec)
    res = pl.pallas_call(
        _post_kernel,
        out_shape=tuple(out_shape),
        grid=(rows // T,),
        in_specs=[row_spec, row_spec, pl.BlockSpec((T, LANES), lambda i: (i, 0)), vec_spec, vec_spec],
        out_specs=tuple(out_specs),
        compiler_params=_cparams(("parallel",)),
        name="post_norm_residual",
    )(h, out, ss, post_w, next_pre_w if with_next else post_w)
    return res if with_next else (res[0], None)


def _split_w_in(w_in):
    sizes = (BRANCH_WIDTH, BRANCH_WIDTH + SSD_BC, SSD_HEADS, MLA_Q_RANK, MLA_KV_RANK, MLA_ROPE,
             BRANCH_WIDTH, BRANCH_WIDTH, BRANCH_WIDTH)
    offs = [0]
    for sz in sizes:
        offs.append(offs[-1] + sz)
    return [w_in[:, offs[i]:offs[i + 1]] for i in range(len(sizes))]


def _layer_weights(w_in, w_gate, w_q_b, w_kv_b):
    z, xbc, dt, q_lat, kv_lat, k_rope, mla_gate, pool_u, pool_gate = _split_w_in(w_in)
    w_big = jnp.concatenate(
        [z, xbc[:, :BRANCH_WIDTH], mla_gate, pool_u, pool_gate, xbc[:, BRANCH_WIDTH:], q_lat, kv_lat,
         w_gate[0], w_gate[1], w_gate[2]], axis=1).astype(BF16)
    half = MLA_ROPE // 2
    d = w_in.shape[0]
    k_sw = jnp.concatenate([-k_rope[:, half:], k_rope[:, :half]], axis=1)
    w_small = jnp.concatenate(
        [dt, jnp.zeros((d, LANES - SSD_HEADS), F32),
         k_rope, jnp.zeros((d, LANES - MLA_ROPE), F32),
         k_sw, jnp.zeros((d, LANES - MLA_ROPE), F32)], axis=1).astype(BF16)
    wq = w_q_b.reshape(MLA_Q_RANK, MLA_HEADS, MLA_NOPE + MLA_ROPE)
    nope = wq[:, :, :MLA_NOPE]
    pe = wq[:, :, MLA_NOPE:]
    pe_sw = jnp.concatenate([-pe[:, :, half:], pe[:, :, :half]], axis=2)
    zpad = jnp.zeros((MLA_Q_RANK, MLA_HEADS, LANES - MLA_ROPE), F32)
    wq2 = jnp.concatenate([nope, pe, zpad, pe_sw, zpad], axis=2).reshape(MLA_Q_RANK, MLA_HEADS * MLA_Q_COLS)
    return w_big, w_small, wq2.astype(BF16), w_kv_b.astype(BF16)


def _rope_tables(batch, seq, rows):
    inv_freq = jnp.power(ROPE_BASE, -jnp.arange(0, MLA_ROPE, 2, dtype=F32) / MLA_ROPE)
    pos_frames = jnp.tile(jnp.arange(seq, dtype=F32) + N_META, batch)
    pos_meta = jnp.arange(rows - batch * seq, dtype=F32)
    pos = jnp.concatenate([pos_frames, pos_meta])
    ang = pos[:, None] * inv_freq[None, :]
    zero = jnp.zeros((rows, LANES - MLA_ROPE), F32)
    cosv = jnp.concatenate([jnp.cos(ang), jnp.cos(ang), zero], axis=1)
    sinv = jnp.concatenate([jnp.sin(ang), jnp.sin(ang), zero], axis=1)
    return cosv, sinv


def kernel(x, meta_tokens, pre_norm_w, w_in, conv_w, conv_b, dt_bias, a_log, d_skip, ssd_norm_w, q_norm_w, w_q_b, kv_norm_w, w_kv_b, w_pool, pool_scale, w_gate, w_branch, w_out, post_norm_w):
    batch, seq, d = x.shape
    depth = w_in.shape[0]
    assert d == D_MODEL and seq % TM_PROJ == 0
    rows = batch * seq + META_PAD
    h = jnp.concatenate([x.reshape(batch * seq, d), meta_tokens.astype(x.dtype),
                         jnp.zeros((META_PAD - N_META, d), x.dtype)], axis=0)
    cosv, sinv = _rope_tables(batch, seq, rows)
    xn = _rmsnorm(h, pre_norm_w[0].reshape(1, d))
    for i in range(depth):
        w_big, w_small, wq, wkv = _layer_weights(w_in[i], w_gate[i], w_q_b[i], w_kv_b[i])
        p, small = _proj(xn, w_big, w_small)
        y_ssd = _ssd(p, small, conv_w[i], conv_b[i], dt_bias[i], a_log[i], d_skip[i], ssd_norm_w[i], batch, seq)
        q, kc, v = _mla_proj(p, small, cosv, sinv, q_norm_w[i], kv_norm_w[i], wq, wkv)
        y_mla = _attention(q, kc, v, p, batch, seq)
        y_pool = _pool(p, w_pool[i], pool_scale[i], batch, seq)
        merged = _merge(y_ssd, y_mla, y_pool, w_branch[i].astype(BF16), p)
        out, ss = _out_proj(merged, w_out[i].astype(BF16))
        next_pre_w = pre_norm_w[i + 1].reshape(1, d) if i + 1 < depth else None
        h, xn = _post_norm_residual(h, out, ss, post_norm_w[i].reshape(1, d), next_pre_w)
    return h[:batch * seq].reshape(batch, seq, d)
```
